```python
import jax, jax.numpy as jnp
from jax import lax
import numpy as np

D_MODEL = 1024
BATCH = 32
SEQ = 256
DEPTH = 2
DEC_BATCH = 2
DEC_SEQ = 2048
PAST_LEN = 256

GRID_W = 64
D_CONV = 512
CONV_K = 31
N_HEADS = 8
QK_NOPE = 64
QK_ROPE = 32
QK_DIM = QK_NOPE + QK_ROPE
V_DIM = 64
Q_LORA = 256
KV_LORA = 128
N_GROUPS = 4
EXPERTS_PER_GROUP = 4
N_EXPERTS = N_GROUPS * EXPERTS_PER_GROUP
TOP_K_IN_GROUP = 2
D_EXPERT = 256
ROPE_BASE = 10000.0
Q_BLOCK = 128
EPS = 1e-6
IN_W = 2 * D_CONV + Q_LORA + KV_LORA + QK_ROPE + 2 * D_MODEL

kernel_name = 'hybrid_conv_mla_hmoe_diffusion_step'


def rms_norm(x, g):
    xf = x.astype(jnp.float32)
    y = xf * lax.rsqrt(jnp.mean(xf * xf, axis=-1, keepdims=True) + EPS)
    return (y * g.astype(jnp.float32)).astype(x.dtype)


def layer_norm(x, g, b):
    xf = x.astype(jnp.float32)
    mu = jnp.mean(xf, axis=-1, keepdims=True)
    var = jnp.mean(jnp.square(xf - mu), axis=-1, keepdims=True)
    y = (xf - mu) * lax.rsqrt(var + EPS)
    return (y * g.astype(jnp.float32) + b.astype(jnp.float32)).astype(x.dtype)


def axial_rope(x, rows, cols):
    half = QK_ROPE // 2
    nf = half // 2
    freqs = ROPE_BASE ** (-jnp.arange(nf, dtype=jnp.float32) / nf)

    def rot(xa, pos):
        ang = pos.astype(jnp.float32)[:, None] * freqs
        cos = jnp.cos(ang)[:, None, :]
        sin = jnp.sin(ang)[:, None, :]
        xf = xa.astype(jnp.float32)
        x1, x2 = xf[..., :nf], xf[..., nf:]
        return jnp.concatenate([x1 * cos - x2 * sin, x2 * cos + x1 * sin], axis=-1)

    out = jnp.concatenate([rot(x[..., :half], rows), rot(x[..., half:], cols)], axis=-1)
    return out.astype(x.dtype)


def rope_heads(x, rows, cols):
    return jnp.concatenate([x[..., :QK_NOPE], axial_rope(x[..., QK_NOPE:], rows, cols)], axis=-1)


def adaln(c_vec, w_ada, b_ada):
    mod = jax.nn.silu(c_vec) @ w_ada + b_ada
    return jnp.split(mod[:, None, :], 6, axis=-1)


def split_projection(proj):
    i0 = 2 * D_CONV
    i1 = i0 + Q_LORA
    i2 = i1 + KV_LORA
    i3 = i2 + QK_ROPE
    return proj[..., :i0], proj[..., i0:i1], proj[..., i1:i2], proj[..., i2:i3], proj[..., i3:]


def conv_module(conv_in, w_dw, b_dw, ln_g, ln_b, w_pw2):
    a, b = jnp.split(conv_in, 2, axis=-1)
    u = a * jax.nn.sigmoid(b)
    u = lax.conv_general_dilated(
        u, w_dw[:, None, :], window_strides=(1,), padding=[(CONV_K // 2, CONV_K // 2)],
        dimension_numbers=('NWC', 'WIO', 'NWC'), feature_group_count=D_CONV) + b_dw
    u = jax.nn.silu(layer_norm(u, ln_g, ln_b))
    return u @ w_pw2


def mla_queries(q_lat, q_lora_g, w_uq, q_norm_g):
    cq = rms_norm(q_lat, q_lora_g)
    q = (cq @ w_uq).reshape(cq.shape[0], cq.shape[1], N_HEADS, QK_DIM)
    return rms_norm(q, q_norm_g)


def mla_keys_values(ckv, krope, w_ukv, k_norm_g):
    bsz, length = ckv.shape[0], ckv.shape[1]
    kv = (ckv @ w_ukv).reshape(bsz, length, N_HEADS, QK_NOPE + V_DIM)
    k_nope, v = kv[..., :QK_NOPE], kv[..., QK_NOPE:]
    k_pe = jnp.broadcast_to(krope[:, :, None, :], (bsz, length, N_HEADS, QK_ROPE))
    k = rms_norm(jnp.concatenate([k_nope, k_pe], axis=-1), k_norm_g)
    return k, v


def attend(q, k, v):
    bsz, s = q.shape[0], q.shape[1]
    nb = s // Q_BLOCK
    qb = q.reshape(bsz, nb, Q_BLOCK, N_HEADS, QK_DIM).transpose(1, 0, 2, 3, 4)
    scale = QK_DIM ** -0.5

    def block(qi):
        sc = jnp.einsum('bqhd,bkhd->bhqk', qi, k).astype(jnp.float32) * scale
        p = jax.nn.softmax(sc, axis=-1).astype(v.dtype)
        return jnp.einsum('bhqk,bkhv->bqhv', p, v)

    o = lax.map(block, qb)
    return o.transpose(1, 0, 2, 3, 4).reshape(bsz, s, N_HEADS * V_DIM)


def hier_moe(h, w_group, b_group, w_expert, b_expert, w1, w3, w2):
    bsz, s, d = h.shape
    t = h.reshape(-1, d)
    g_logits = (t @ w_group + b_group).astype(jnp.float32)
    g_prob = jax.nn.softmax(g_logits, axis=-1)
    g_idx = jnp.argmax(g_logits, axis=-1)
    g_w = jnp.take_along_axis(g_prob, g_idx[:, None], axis=-1)
    e_logits = (t @ w_expert + b_expert).astype(jnp.float32).reshape(-1, N_GROUPS, EXPERTS_PER_GROUP)
    e_sel = jnp.take_along_axis(e_logits, g_idx[:, None, None], axis=1)[:, 0]
    e_prob = jax.nn.softmax(e_sel, axis=-1)
    top_v, top_i = lax.top_k(e_prob, TOP_K_IN_GROUP)
    top_v = top_v / jnp.sum(top_v, axis=-1, keepdims=True)
    ids = g_idx[:, None] * EXPERTS_PER_GROUP + top_i
    comb = jnp.sum(jax.nn.one_hot(ids, N_EXPERTS, dtype=jnp.float32) * (g_w * top_v)[..., None], axis=1)
    hid = jax.nn.silu(jnp.einsum('td,edf->tef', t, w1)) * jnp.einsum('td,edf->tef', t, w3)
    hid = hid * comb.astype(hid.dtype)[..., None]
    y = jnp.einsum('tef,efd->td', hid, w2)
    return y.reshape(bsz, s, d)


def trunk_layer(x, c_vec, lp, ctx_cache, pos):
    shift1, scale1, gate1, shift2, scale2, gate2 = adaln(c_vec, lp['w_ada'], lp['b_ada'])
    h = rms_norm(x, lp['norm1']) * (1 + scale1) + shift1
    conv_in, q_lat, kv_lat, k_rope, gates = split_projection(h @ lp['w_in'])
    conv_out = conv_module(conv_in, lp['conv_dw'], lp['conv_dw_b'], lp['conv_ln_g'], lp['conv_ln_b'], lp['w_pw2'])
    q = mla_queries(q_lat, lp['q_lora_norm'], lp['w_uq'], lp['q_norm'])
    ckv = rms_norm(kv_lat, lp['kv_lora_norm'])
    k, v = mla_keys_values(ckv, k_rope, lp['w_ukv'], lp['k_norm'])
    if ctx_cache is not None:
        rows, cols = pos
        q = rope_heads(q, rows, cols)
        k = rope_heads(k, rows, cols)
        k_ctx, v_ctx = mla_keys_values(ctx_cache[0], ctx_cache[1], lp['w_ukv'], lp['k_norm'])
        k = jnp.concatenate([k, k_ctx], axis=1)
        v = jnp.concatenate([v, v_ctx], axis=1)
    att = attend(q, k, v) @ lp['w_o']
    g_conv, g_att = jnp.split(jax.nn.sigmoid(gates), 2, axis=-1)
    x = x + gate1 * ((g_conv * conv_out + g_att * att) @ lp['w_out'])
    h2 = rms_norm(x, lp['norm2']) * (1 + scale2) + shift2
    x = x + gate2 * hier_moe(h2, lp['w_group'], lp['b_group'], lp['w_expert'], lp['b_expert'],
                             lp['w1'], lp['w3'], lp['w2'])
    return x, ckv, k_rope


def setup_inputs(seed: int = 0) -> dict:
    key = jax.random.key(seed)
    ks = jax.random.split(key, 32)

    def nrm(k, shape, scale=1.0):
        return jax.random.normal(k, shape, dtype=jnp.float32) * scale

    L = DEPTH
    return {
        'x_prompt': nrm(ks[0], (BATCH, SEQ, D_MODEL)),
        'x_sample': nrm(ks[1], (DEC_BATCH, DEC_SEQ, D_MODEL)),
        'cache_ckv': nrm(ks[2], (DEC_BATCH, DEPTH, PAST_LEN, KV_LORA)),
        'cache_krope': nrm(ks[3], (DEC_BATCH, DEPTH, PAST_LEN, QK_ROPE)),
        'c': nrm(ks[4], (DEC_BATCH, D_MODEL)),
        'c_ctx': nrm(ks[5], (D_MODEL,)),
        'w_ada': nrm(ks[6], (L, D_MODEL, 6 * D_MODEL), 0.5 * D_MODEL ** -0.5),
        'b_ada': nrm(ks[7], (L, 6 * D_MODEL), 0.01),
        'norm1': 1.0 + nrm(ks[8], (L, D_MODEL), 0.01),
        'w_in': nrm(ks[9], (L, D_MODEL, IN_W), D_MODEL ** -0.5),
        'q_lora_norm': 1.0 + nrm(ks[10], (L, Q_LORA), 0.01),
        'w_uq': nrm(ks[11], (L, Q_LORA, N_HEADS * QK_DIM), Q_LORA ** -0.5),
        'kv_lora_norm': 1.0 + nrm(ks[12], (L, KV_LORA), 0.01),
        'w_ukv': nrm(ks[13], (L, KV_LORA, N_HEADS * (QK_NOPE + V_DIM)), KV_LORA ** -0.5),
        'q_norm': 1.0 + nrm(ks[14], (L, QK_DIM), 0.01),
        'k_norm': 1.0 + nrm(ks[15], (L, QK_DIM), 0.01),
        'w_o': nrm(ks[16], (L, N_HEADS * V_DIM, D_MODEL), (N_HEADS * V_DIM) ** -0.5),
        'conv_dw': nrm(ks[17], (L, CONV_K, D_CONV), CONV_K ** -0.5),
        'conv_dw_b': nrm(ks[18], (L, D_CONV), 0.01),
        'conv_ln_g': 1.0 + nrm(ks[19], (L, D_CONV), 0.01),
        'conv_ln_b': nrm(ks[20], (L, D_CONV), 0.01),
        'w_pw2': nrm(ks[21], (L, D_CONV, D_MODEL), D_CONV ** -0.5),
        'w_out': nrm(ks[22], (L, D_MODEL, D_MODEL), D_MODEL ** -0.5),
        'norm2': 1.0 + nrm(ks[23], (L, D_MODEL), 0.01),
        'w_group': nrm(ks[24], (L, D_MODEL, N_GROUPS), D_MODEL ** -0.5),
        'b_group': nrm(ks[25], (L, N_GROUPS), 0.01),
        'w_expert': nrm(ks[26], (L, D_MODEL, N_EXPERTS), D_MODEL ** -0.5),
        'b_expert': nrm(ks[27], (L, N_EXPERTS), 0.01),
        'w1': nrm(ks[28], (L, N_EXPERTS, D_MODEL, D_EXPERT), D_MODEL ** -0.5),
        'w3': nrm(ks[29], (L, N_EXPERTS, D_MODEL, D_EXPERT), D_MODEL ** -0.5),
        'w2': nrm(ks[30], (L, N_EXPERTS, D_EXPERT, D_MODEL), D_EXPERT ** -0.5),
    }


def reference(x_prompt, x_sample, cache_ckv, cache_krope, c, c_ctx, w_ada, b_ada, norm1, w_in,
              q_lora_norm, w_uq, kv_lora_norm, w_ukv, q_norm, k_norm, w_o, conv_dw, conv_dw_b,
              conv_ln_g, conv_ln_b, w_pw2, w_out, norm2, w_group, b_group, w_expert, b_expert,
              w1, w3, w2):
    n_lat = x_sample.shape[1]
    n_rows = n_lat // GRID_W
    rows = jnp.repeat(jnp.arange(n_rows, dtype=jnp.int32), GRID_W)
    cols = jnp.tile(jnp.arange(GRID_W, dtype=jnp.int32), n_rows)
    c_ctx_vec = c_ctx[None, :]

    xp, xs = x_prompt, x_sample
    ckv_list, krope_list = [], []
    for l in range(DEPTH):
        lp = {
            'w_ada': w_ada[l], 'b_ada': b_ada[l], 'norm1': norm1[l], 'w_in': w_in[l],
            'q_lora_norm': q_lora_norm[l], 'w_uq': w_uq[l], 'kv_lora_norm': kv_lora_norm[l],
            'w_ukv': w_ukv[l], 'q_norm': q_norm[l], 'k_norm': k_norm[l], 'w_o': w_o[l],
            'conv_dw': conv_dw[l], 'conv_dw_b': conv_dw_b[l], 'conv_ln_g': conv_ln_g[l],
            'conv_ln_b': conv_ln_b[l], 'w_pw2': w_pw2[l], 'w_out': w_out[l], 'norm2': norm2[l],
            'w_group': w_group[l], 'b_group': b_group[l], 'w_expert': w_expert[l],
            'b_expert': b_expert[l], 'w1': w1[l], 'w3': w3[l], 'w2': w2[l],
        }
        xp, ckv_l, krope_l = trunk_layer(xp, c_ctx_vec, lp, None, None)
        ckv_list.append(ckv_l)
        krope_list.append(krope_l)
        xs, _, _ = trunk_layer(xs, c, lp, (cache_ckv[:, l], cache_krope[:, l]), (rows, cols))
    new_ckv = jnp.stack(ckv_list, axis=1)
    new_krope = jnp.stack(krope_list, axis=1)
    return (xp, xs, new_ckv, new_krope)
```

```python
import functools

import jax
import jax.numpy as jnp
import numpy as np
from jax import lax
from jax.experimental import pallas as pl
from jax.experimental.pallas import tpu as pltpu

D_MODEL = 1024
BATCH = 32
SEQ = 256
DEPTH = 2
DEC_BATCH = 2
DEC_SEQ = 2048
PAST_LEN = 256
GRID_W = 64
D_CONV = 512
CONV_K = 31
N_HEADS = 8
QK_NOPE = 64
QK_ROPE = 32
QK_DIM = QK_NOPE + QK_ROPE
V_DIM = 64
Q_LORA = 256
KV_LORA = 128
N_GROUPS = 4
EXPERTS_PER_GROUP = 4
N_EXPERTS = N_GROUPS * EXPERTS_PER_GROUP
D_EXPERT = 256
ROPE_BASE = 10000.0
EPS = 1e-6

LANES = 128
HALO = 16
T_CTX = BATCH * SEQ
T_LAT = DEC_BATCH * DEC_SEQ
T_ALL = T_CTX + T_LAT
TM = 256
N_CTX_TILES = T_CTX // TM
LAT_TILES = DEC_SEQ // TM
N_TILES = T_ALL // TM
HD = N_HEADS * LANES
G3_W = 256
W_IN_EXT = 2 * D_CONV + Q_LORA + G3_W + 2 * D_MODEL
VMEM_LIMIT = 56 * 1024 * 1024

F32 = jnp.float32
BF16 = jnp.bfloat16


def _dot(a, b):
    return jnp.dot(a, b, preferred_element_type=F32)


def _rms(x, g):
    ms = jnp.mean(x * x, axis=-1, keepdims=True)
    return x * lax.rsqrt(ms + EPS) * g


def _sigmoid(x):
    return 1.0 / (1.0 + jnp.exp(-x))


def _mod_row(i):
    return jnp.where(i < N_CTX_TILES, 0, 1 + (i - N_CTX_TILES) // LAT_TILES)


def _head_norm_rope(xp, gain, tab):
    lane = lax.broadcasted_iota(jnp.int32, xp.shape, 1)
    ss = jnp.sum(jnp.where(lane < QK_DIM, xp * xp, 0.0), axis=-1, keepdims=True)
    t = xp * lax.rsqrt(ss * (1.0 / QK_DIM) + EPS) * gain * tab
    rolled = pltpu.roll(t, LANES - QK_ROPE, 1)
    return jnp.where(lane < QK_NOPE, t, jnp.where(lane < QK_DIM, t + rolled, 0.0))


def _adaln_kernel(c_ref, w_ref, b_ref, o_ref):
    c = c_ref[...]
    s = c * _sigmoid(c)
    o_ref[0] = jnp.dot(s, w_ref[0], preferred_element_type=F32, precision=lax.Precision.HIGHEST) + b_ref[0]


def _adaln(cvec, w_ada, b_ada):
    nblk = 6
    return pl.pallas_call(
        _adaln_kernel,
        grid=(DEPTH, nblk),
        in_specs=[
            pl.BlockSpec((8, D_MODEL), lambda l, j: (0, 0)),
            pl.BlockSpec((1, D_MODEL, D_MODEL), lambda l, j: (l, 0, j)),
            pl.BlockSpec((1, 1, D_MODEL), lambda l, j: (l, 0, j)),
        ],
        out_specs=pl.BlockSpec((1, 8, D_MODEL), lambda l, j: (l, 0, j)),
        out_shape=jax.ShapeDtypeStruct((DEPTH, 8, 6 * D_MODEL), F32),
        compiler_params=pltpu.CompilerParams(vmem_limit_bytes=VMEM_LIMIT),
        name="adaln",
    )(cvec, w_ada, b_ada.reshape(DEPTH, 1, 6 * D_MODEL))


def _front_kernel(x_ref, mod_ref, n1_ref, win_ref, qln_ref, wuq_ref, kvn_ref, wkv_ref, qg_ref, kg_ref, tab_ref,
                  u_ref, q_ref, k_ref, v_ref, g_ref, ckv_ref, kr_ref):
    x = x_ref[...]
    mod = mod_ref[0]
    shift1 = mod[:, 0:D_MODEL]
    scale1 = mod[:, D_MODEL:2 * D_MODEL]
    hb = (_rms(x, n1_ref[...]) * (1.0 + scale1) + shift1).astype(BF16)

    ab = _dot(hb, win_ref[:, 0:2 * D_CONV])
    u_ref[...] = ab[:, :D_CONV] * _sigmoid(ab[:, D_CONV:])

    o1 = 2 * D_CONV
    cq = _rms(_dot(hb, win_ref[:, o1:o1 + Q_LORA]), qln_ref[...]).astype(BF16)
    qp = _dot(cq, wuq_ref[...])

    o2 = o1 + Q_LORA
    g3 = _dot(hb, win_ref[:, o2:o2 + G3_W])
    ckv = _rms(g3[:, :KV_LORA], kvn_ref[...])
    ckv_ref[...] = ckv
    kr_ref[...] = g3[:, KV_LORA:KV_LORA + QK_ROPE]
    lhs = jnp.concatenate([ckv, g3[:, KV_LORA:]], axis=-1).astype(BF16)
    kv = _dot(lhs, wkv_ref[...])
    v_ref[...] = kv[:, HD:].astype(BF16)

    tab = tab_ref[...]
    qg = qg_ref[...]
    kg = kg_ref[...]
    for h in range(N_HEADS):
        sl = slice(h * LANES, (h + 1) * LANES)
        q_ref[:, sl] = _head_norm_rope(qp[:, sl], qg, tab).astype(BF16)
        k_ref[:, sl] = _head_norm_rope(kv[:, sl], kg, tab).astype(BF16)

    o3 = o2 + G3_W
    g_ref[...] = _sigmoid(_dot(hb, win_ref[:, o3:o3 + 2 * D_MODEL])).astype(BF16)


def _front(x, mod_l, n1, win, qln, wuq, kvn, wkv, qg, kg, tab):
    const = lambda i: (0, 0)
    row = lambda i: (i, 0)
    return pl.pallas_call(
        _front_kernel,
        grid=(N_TILES,),
        in_specs=[
            pl.BlockSpec((TM, D_MODEL), row),
            pl.BlockSpec((1, 1, 6 * D_MODEL), lambda i: (_mod_row(i), 0, 0)),
            pl.BlockSpec((1, D_MODEL), const),
            pl.BlockSpec((D_MODEL, W_IN_EXT), const),
            pl.BlockSpec((1, Q_LORA), const),
            pl.BlockSpec((Q_LORA, HD), const),
            pl.BlockSpec((1, KV_LORA), const),
            pl.BlockSpec((G3_W, HD + N_HEADS * V_DIM), const),
            pl.BlockSpec((1, LANES), const),
            pl.BlockSpec((1, LANES), const),
            pl.BlockSpec((TM, LANES), lambda i: (jnp.where(i < N_CTX_TILES, 0, 1 + (i - N_CTX_TILES) % LAT_TILES), 0)),
        ],
        out_specs=[
            pl.BlockSpec((TM, D_CONV), row),
            pl.BlockSpec((TM, HD), row),
            pl.BlockSpec((TM, HD), row),
            pl.BlockSpec((TM, N_HEADS * V_DIM), row),
            pl.BlockSpec((TM, 2 * D_MODEL), row),
            pl.BlockSpec((TM, KV_LORA), row),
            pl.BlockSpec((TM, QK_ROPE), row),
        ],
        out_shape=[
            jax.ShapeDtypeStruct((T_ALL, D_CONV), F32),
            jax.ShapeDtypeStruct((T_ALL, HD), BF16),
            jax.ShapeDtypeStruct((T_ALL, HD), BF16),
            jax.ShapeDtypeStruct((T_ALL, N_HEADS * V_DIM), BF16),
            jax.ShapeDtypeStruct((T_ALL, 2 * D_MODEL), BF16),
            jax.ShapeDtypeStruct((T_ALL, KV_LORA), F32),
            jax.ShapeDtypeStruct((T_ALL, QK_ROPE), F32),
        ],
        compiler_params=pltpu.CompilerParams(dimension_semantics=("arbitrary",), vmem_limit_bytes=VMEM_LIMIT),
        name="front",
    )(x, mod_l, n1, win, qln, wuq, kvn, wkv, qg, kg, tab)


def _cache_kv_kernel(c_ref, wkv_ref, kg_ref, k_ref, v_ref):
    kv = _dot(c_ref[0, 0].astype(BF16), wkv_ref[0])
    v_ref[0, 0] = kv[:, HD:].astype(BF16)
    kg = kg_ref[0]
    ident = jnp.where(lax.broadcasted_iota(jnp.int32, (PAST_LEN, LANES), 1) < QK_NOPE + QK_ROPE, 1.0, 0.0)
    for h in range(N_HEADS):
        sl = slice(h * LANES, (h + 1) * LANES)
        k_ref[0, 0, :, sl] = _head_norm_rope(kv[:, sl], kg, ident).astype(BF16)


def _cache_kv(cache_cat, wkv, kg):
    return pl.pallas_call(
        _cache_kv_kernel,
        grid=(DEPTH, DEC_BATCH),
        in_specs=[
            pl.BlockSpec((1, 1, PAST_LEN, G3_W), lambda l, b: (l, b, 0, 0)),
            pl.BlockSpec((1, G3_W, HD + N_HEADS * V_DIM), lambda l, b: (l, 0, 0)),
            pl.BlockSpec((1, 1, LANES), lambda l, b: (l, 0, 0)),
        ],
        out_specs=[
            pl.BlockSpec((1, 1, PAST_LEN, HD), lambda l, b: (l, b, 0, 0)),
            pl.BlockSpec((1, 1, PAST_LEN, N_HEADS * V_DIM), lambda l, b: (l, b, 0, 0)),
        ],
        out_shape=[
            jax.ShapeDtypeStruct((DEPTH, DEC_BATCH, PAST_LEN, HD), BF16),
            jax.ShapeDtypeStruct((DEPTH, DEC_BATCH, PAST_LEN, N_HEADS * V_DIM), BF16),
        ],
        compiler_params=pltpu.CompilerParams(vmem_limit_bytes=VMEM_LIMIT),
        name="cache_kv",
    )(cache_cat, wkv, kg)


CONV_RB = 64


def _conv_kernel(up_ref, uc_ref, un_ref, g_ref, wdw_ref, bdw_ref, lng_ref, lnb_ref, wpw_ref, o_ref, pad_ref, cv_ref):
    i = pl.program_id(0)
    j = (i - N_CTX_TILES) % LAT_TILES
    is_lat = i >= N_CTX_TILES
    use_prev = jnp.logical_and(is_lat, j != 0)
    use_next = jnp.logical_and(is_lat, j != LAT_TILES - 1)
    pad_ref[0:HALO, :] = jnp.where(use_prev, up_ref[...], 0.0)
    pad_ref[HALO:HALO + TM, :] = uc_ref[...]
    pad_ref[HALO + TM:, :] = jnp.where(use_next, un_ref[...], 0.0)

    base = HALO - CONV_K // 2
    for c in range(D_CONV // LANES):
        cs = slice(c * LANES, (c + 1) * LANES)
        for r in range(TM // CONV_RB):
            acc = jnp.zeros((CONV_RB, LANES), F32)
            for k in range(CONV_K):
                acc = acc + pad_ref[base + r * CONV_RB + k:base + r * CONV_RB + k + CONV_RB, cs] * wdw_ref[k:k + 1, cs]
            cv_ref[r * CONV_RB:(r + 1) * CONV_RB, cs] = acc

    t = cv_ref[...] + bdw_ref[...]
    mu = jnp.mean(t, axis=-1, keepdims=True)
    d = t - mu
    var = jnp.mean(d * d, axis=-1, keepdims=True)
    y = d * lax.rsqrt(var + EPS) * lng_ref[...] + lnb_ref[...]
    y = (y * _sigmoid(y)).astype(BF16)
    o_ref[...] = (g_ref[...].astype(F32) * _dot(y, wpw_ref[...])).astype(BF16)


def _conv(u, gates, wdw, bdw, lng, lnb, wpw):
    nh = TM // HALO
    const = lambda i: (0, 0)
    return pl.pallas_call(
        _conv_kernel,
        grid=(N_TILES,),
        in_specs=[
            pl.BlockSpec((HALO, D_CONV), lambda i: (jnp.maximum(i * nh - 1, 0), 0)),
            pl.BlockSpec((TM, D_CONV), lambda i: (i, 0)),
            pl.BlockSpec((HALO, D_CONV), lambda i: (jnp.minimum((i + 1) * nh, T_ALL // HALO - 1), 0)),
            pl.BlockSpec((TM, D_MODEL), lambda i: (i, 0)),
            pl.BlockSpec((32, D_CONV), const),
            pl.BlockSpec((1, D_CONV), const),
            pl.BlockSpec((1, D_CONV), const),
            pl.BlockSpec((1, D_CONV), const),
            pl.BlockSpec((D_CONV, D_MODEL), const),
        ],
        out_specs=pl.BlockSpec((TM, D_MODEL), lambda i: (i, 0)),
        out_shape=jax.ShapeDtypeStruct((T_ALL, D_MODEL), BF16),
        scratch_shapes=[pltpu.VMEM((TM + 2 * HALO, D_CONV), F32), pltpu.VMEM((TM, D_CONV), F32)],
        compiler_params=pltpu.CompilerParams(dimension_semantics=("arbitrary",), vmem_limit_bytes=VMEM_LIMIT),
        name="conv",
    )(u, u, u, gates, wdw, bdw, lng, lnb, wpw)


def _attn_kernel(*refs, n_src):
    q_ref = refs[0]
    k_refs = refs[1:1 + n_src]
    v_refs = refs[1 + n_src:1 + 2 * n_src]
    o_ref = refs[1 + 2 * n_src]
    nt = (((1,), (1,)), ((), ()))
    outs = []
    for pair in range(N_HEADS // 2):
        ps = slice(pair * LANES, (pair + 1) * LANES)
        v_lo, v_hi = [], []
        for v_ref in v_refs:
            vp = v_ref[:, ps]
            lane = lax.broadcasted_iota(jnp.int32, vp.shape, 1)
            v_lo.append(jnp.where(lane < V_DIM, vp, jnp.zeros_like(vp)))
            v_hi.append(jnp.where(lane < V_DIM, jnp.zeros_like(vp), vp))
        o_pair = None
        for half, vs in ((0, v_lo), (1, v_hi)):
            hs = slice((2 * pair + half) * LANES, (2 * pair + half + 1) * LANES)
            qh = q_ref[:, hs]
            s = [lax.dot_general(qh, k_ref[:, hs], nt, preferred_element_type=F32) for k_ref in k_refs]
            m = functools.reduce(jnp.maximum, [jnp.max(x, axis=-1, keepdims=True) for x in s])
            p = [jnp.exp(x - m) for x in s]
            den = functools.reduce(jnp.add, [jnp.sum(x, axis=-1, keepdims=True) for x in p])
            o = functools.reduce(jnp.add, [_dot(x.astype(BF16), v) for x, v in zip(p, vs)])
            o = o * (1.0 / den)
            o_pair = o if o_pair is None else o_pair + o
        outs.append(o_pair)
    o_ref[...] = jnp.concatenate(outs, axis=-1).astype(BF16)


def _attn_ctx(q, k, v):
    spec = lambda w: pl.BlockSpec((SEQ, w), lambda b: (b, 0))
    return pl.pallas_call(
        functools.partial(_attn_kernel, n_src=1),
        grid=(BATCH,),
        in_specs=[spec(HD), spec(HD), spec(N_HEADS * V_DIM)],
        out_specs=spec(N_HEADS * V_DIM),
        out_shape=jax.ShapeDtypeStruct((T_CTX, N_HEADS * V_DIM), BF16),
        compiler_params=pltpu.CompilerParams(dimension_semantics=("arbitrary",), vmem_limit_bytes=VMEM_LIMIT),
        name="attn_ctx",
    )(q, k, v)


TQ = 256


def _attn_lat(q, k, v, kc, vc):
    nq = DEC_SEQ // TQ
    q0 = T_CTX // TQ
    b0 = T_CTX // DEC_SEQ
    vw = N_HEADS * V_DIM
    return pl.pallas_call(
        functools.partial(_attn_kernel, n_src=2),
        grid=(DEC_BATCH, nq),
        in_specs=[
            pl.BlockSpec((TQ, HD), lambda b, i: (q0 + b * nq + i, 0)),
            pl.BlockSpec((DEC_SEQ, HD), lambda b, i: (b0 + b, 0)),
            pl.BlockSpec((None, PAST_LEN, HD), lambda b, i: (b, 0, 0)),
            pl.BlockSpec((DEC_SEQ, vw), lambda b, i: (b0 + b, 0)),
            pl.BlockSpec((None, PAST_LEN, vw), lambda b, i: (b, 0, 0)),
        ],
        out_specs=pl.BlockSpec((TQ, vw), lambda b, i: (b * nq + i, 0)),
        out_shape=jax.ShapeDtypeStruct((T_LAT, vw), BF16),
        compiler_params=pltpu.CompilerParams(dimension_semantics=("arbitrary", "arbitrary"),
                                             vmem_limit_bytes=VMEM_LIMIT),
        name="attn_lat",
    )(q, k, kc, v, vc)


R_LANES = 128


def _route(logits):
    lane = lax.broadcasted_iota(jnp.int32, logits.shape, 1)
    lane_f = lane.astype(F32)
    neg = -jnp.inf
    big = float(R_LANES)
    gmask = jnp.logical_and(lane >= N_EXPERTS, lane < N_EXPERTS + N_GROUPS)
    gl = jnp.where(gmask, logits, neg)
    gmax = jnp.max(gl, axis=-1, keepdims=True)
    gsum = jnp.sum(jnp.where(gmask, jnp.exp(gl - gmax), 0.0), axis=-1, keepdims=True)
    g_w = 1.0 / gsum
    g_lane = jnp.min(jnp.where(gl == gmax, lane_f, big), axis=-1, keepdims=True)
    g_idx = g_lane - float(N_EXPERTS)
    e_lo = g_idx * float(EXPERTS_PER_GROUP)
    emask = jnp.logical_and(lane_f >= e_lo, lane_f < e_lo + float(EXPERTS_PER_GROUP))
    el = jnp.where(emask, logits, neg)
    e1 = jnp.max(el, axis=-1, keepdims=True)
    i1 = jnp.min(jnp.where(el == e1, lane_f, big), axis=-1, keepdims=True)
    el2 = jnp.where(lane_f == i1, neg, el)
    e2 = jnp.max(el2, axis=-1, keepdims=True)
    i2 = jnp.min(jnp.where(el2 == e2, lane_f, big), axis=-1, keepdims=True)
    r = jnp.exp(e2 - e1)
    w1 = g_w / (1.0 + r)
    w2 = g_w * r / (1.0 + r)
    return jnp.where(lane_f == i1, w1, jnp.where(lane_f == i2, w2, 0.0))


def _post_kernel(x_ref, cg_ref, ah_ref, ga_ref, mod_ref, wo_ref, wout_ref, n2_ref, wr_ref, br_ref,
                 x1_ref, h2_ref, comb_ref):
    mod = mod_ref[0]
    gate1 = mod[:, 2 * D_MODEL:3 * D_MODEL]
    shift2 = mod[:, 3 * D_MODEL:4 * D_MODEL]
    scale2 = mod[:, 4 * D_MODEL:5 * D_MODEL]
    att = _dot(ah_ref[...], wo_ref[...])
    merged = (cg_ref[...].astype(F32) + ga_ref[...].astype(F32) * att).astype(BF16)
    x1 = x_ref[...] + gate1 * _dot(merged, wout_ref[...])
    x1_ref[...] = x1
    h2 = _rms(x1, n2_ref[...]) * (1.0 + scale2) + shift2
    h2_ref[...] = h2.astype(BF16)
    logits = jnp.dot(h2, wr_ref[...], preferred_element_type=F32, precision=lax.Precision.HIGHEST) + br_ref[...]
    comb_ref[...] = _route(logits)


def _post(x, cg, ah, gates, mod_l, wo, wout, n2, wr, br):
    const = lambda i: (0, 0)
    row = lambda i: (i, 0)
    return pl.pallas_call(
        _post_kernel,
        grid=(N_TILES,),
        in_specs=[
            pl.BlockSpec((TM, D_MODEL), row),
            pl.BlockSpec((TM, D_MODEL), row),
            pl.BlockSpec((TM, N_HEADS * V_DIM), row),
            pl.BlockSpec((TM, D_MODEL), lambda i: (i, 1)),
            pl.BlockSpec((1, 1, 6 * D_MODEL), lambda i: (_mod_row(i), 0, 0)),
            pl.BlockSpec((N_HEADS * V_DIM, D_MODEL), const),
            pl.BlockSpec((D_MODEL, D_MODEL), const),
            pl.BlockSpec((1, D_MODEL), const),
            pl.BlockSpec((D_MODEL, R_LANES), const),
            pl.BlockSpec((1, R_LANES), const),
        ],
        out_specs=[
            pl.BlockSpec((TM, D_MODEL), row),
            pl.BlockSpec((TM, D_MODEL), row),
            pl.BlockSpec((TM, R_LANES), row),
        ],
        out_shape=[
            jax.ShapeDtypeStruct((T_ALL, D_MODEL), F32),
            jax.ShapeDtypeStruct((T_ALL, D_MODEL), BF16),
            jax.ShapeDtypeStruct((T_ALL, R_LANES), F32),
        ],
        compiler_params=pltpu.CompilerParams(dimension_semantics=("arbitrary",), vmem_limit_bytes=VMEM_LIMIT),
        name="post",
    )(x, cg, ah, gates, mod_l, wo, wout, n2, wr, br)


def _moe_kernel(x1_ref, h2_ref, comb_ref, mod_ref, w13_ref, w2_ref, o_ref, hid_ref):
    gate2 = mod_ref[0][:, 5 * D_MODEL:6 * D_MODEL]
    h2 = h2_ref[...]
    comb = comb_ref[...]
    for e in range(N_EXPERTS):
        ab = _dot(h2, w13_ref[e])
        a = ab[:, :D_EXPERT]
        hid = a * _sigmoid(a) * ab[:, D_EXPERT:] * comb[:, e:e + 1]
        hid_ref[:, e * D_EXPERT:(e + 1) * D_EXPERT] = hid.astype(BF16)
    o_ref[...] = x1_ref[...] + gate2 * _dot(hid_ref[...], w2_ref[...])


def _moe(x1, h2, comb, mod_l, w13, w2):
    row = lambda i: (i, 0)
    return pl.pallas_call(
        _moe_kernel,
        grid=(N_TILES,),
        in_specs=[
            pl.BlockSpec((TM, D_MODEL), row),
            pl.BlockSpec((TM, D_MODEL), row),
            pl.BlockSpec((TM, R_LANES), row),
            pl.BlockSpec((1, 1, 6 * D_MODEL), lambda i: (_mod_row(i), 0, 0)),
            pl.BlockSpec((N_EXPERTS, D_MODEL, 2 * D_EXPERT), lambda i: (0, 0, 0), pipeline_mode=pl.Buffered(1)),
            pl.BlockSpec((N_EXPERTS * D_EXPERT, D_MODEL), lambda i: (0, 0), pipeline_mode=pl.Buffered(1)),
        ],
        out_specs=pl.BlockSpec((TM, D_MODEL), row),
        out_shape=jax.ShapeDtypeStruct((T_ALL, D_MODEL), F32),
        scratch_shapes=[pltpu.VMEM((TM, N_EXPERTS * D_EXPERT), BF16)],
        compiler_params=pltpu.CompilerParams(dimension_semantics=("arbitrary",), vmem_limit_bytes=VMEM_LIMIT),
        name="moe",
    )(x1, h2, comb, mod_l, w13, w2)


_SWAP = np.concatenate([np.arange(8) + 8, np.arange(8), np.arange(8) + 24, np.arange(8) + 16])


def _rope_table():
    nf = QK_ROPE // 4
    freqs = ROPE_BASE ** (-np.arange(nf, dtype=np.float64) / nf)
    pos = np.arange(DEC_SEQ)
    rows = (pos // GRID_W)[:, None] * freqs
    cols = (pos % GRID_W)[:, None] * freqs
    cos = np.concatenate([np.cos(rows), np.cos(rows), np.cos(cols), np.cos(cols)], axis=1)
    sin = np.concatenate([-np.sin(rows), np.sin(rows), -np.sin(cols), np.sin(cols)], axis=1)
    lat = np.concatenate([np.ones((DEC_SEQ, QK_NOPE)), cos, sin], axis=1)
    ident = np.concatenate([np.ones((TM, QK_NOPE + QK_ROPE)), np.zeros((TM, QK_ROPE))], axis=1)
    return jnp.asarray(np.concatenate([ident, lat], axis=0), dtype=F32)


def _ext_gain(g, scale):
    return (jnp.concatenate([g, g[QK_NOPE:][_SWAP]]) * scale).reshape(1, LANES)


def _layer_weights(l, w_in, q_lora_norm, w_uq, kv_lora_norm, w_ukv, q_norm, k_norm, w_o, conv_dw, conv_dw_b,
                   conv_ln_g, conv_ln_b, w_pw2, w_out, norm1, norm2, w_group, b_group, w_expert, b_expert, w1, w3, w2):
    i0 = 2 * D_CONV
    i1 = i0 + Q_LORA
    i3 = i1 + KV_LORA + QK_ROPE
    wi = w_in[l]
    win = jnp.concatenate(
        [wi[:, :i3], jnp.zeros((D_MODEL, G3_W - KV_LORA - QK_ROPE), F32), wi[:, i3:]], axis=1).astype(BF16)
    uq = w_uq[l].reshape(Q_LORA, N_HEADS, QK_DIM)
    wuq = jnp.concatenate([uq, uq[..., QK_NOPE:][..., _SWAP]], axis=-1).reshape(Q_LORA, HD).astype(BF16)
    ukv = w_ukv[l].reshape(KV_LORA, N_HEADS, QK_NOPE + V_DIM)
    k_top = jnp.concatenate([ukv[..., :QK_NOPE], jnp.zeros((KV_LORA, N_HEADS, LANES - QK_NOPE), F32)], axis=-1)
    eye = np.eye(QK_ROPE, dtype=np.float32)
    place = np.concatenate([np.zeros((QK_ROPE, QK_NOPE), np.float32), eye, eye[:, _SWAP]], axis=1)
    k_mid = jnp.asarray(np.tile(place[:, None, :], (1, N_HEADS, 1)))
    k_rows = jnp.concatenate(
        [k_top, k_mid, jnp.zeros((G3_W - KV_LORA - QK_ROPE, N_HEADS, LANES), F32)], axis=0).reshape(G3_W, HD)
    v_rows = jnp.concatenate(
        [ukv[..., QK_NOPE:].reshape(KV_LORA, N_HEADS * V_DIM),
         jnp.zeros((G3_W - KV_LORA, N_HEADS * V_DIM), F32)], axis=0)
    wkv = jnp.concatenate([k_rows, v_rows], axis=1).astype(BF16)
    wr = jnp.concatenate(
        [w_expert[l], w_group[l], jnp.zeros((D_MODEL, R_LANES - N_EXPERTS - N_GROUPS), F32)], axis=1)
    br = jnp.concatenate(
        [b_expert[l], b_group[l], jnp.zeros((R_LANES - N_EXPERTS - N_GROUPS,), F32)]).reshape(1, R_LANES)
    return dict(
        n1=norm1[l].reshape(1, D_MODEL), win=win, qln=q_lora_norm[l].reshape(1, Q_LORA), wuq=wuq,
        kvn=kv_lora_norm[l].reshape(1, KV_LORA), wkv=wkv,
        qg=_ext_gain(q_norm[l], QK_DIM ** -0.5), kg=_ext_gain(k_norm[l], 1.0),
        wdw=jnp.concatenate([conv_dw[l], jnp.zeros((1, D_CONV), F32)], axis=0),
        bdw=conv_dw_b[l].reshape(1, D_CONV), lng=conv_ln_g[l].reshape(1, D_CONV), lnb=conv_ln_b[l].reshape(1, D_CONV),
        wpw=w_pw2[l].astype(BF16), wo=w_o[l].astype(BF16), wout=w_out[l].astype(BF16),
        n2=norm2[l].reshape(1, D_MODEL), wr=wr, br=br,
        w13=jnp.concatenate([w1[l], w3[l]], axis=-1).astype(BF16),
        w2=w2[l].reshape(N_EXPERTS * D_EXPERT, D_MODEL).astype(BF16),
    )


def kernel(x_prompt, x_sample, cache_ckv, cache_krope, c, c_ctx, w_ada, b_ada, norm1, w_in, q_lora_norm, w_uq, kv_lora_norm, w_ukv, q_norm, k_norm, w_o, conv_dw, conv_dw_b, conv_ln_g, conv_ln_b, w_pw2, w_out, norm2, w_group, b_group, w_expert, b_expert, w1, w3, w2):
    cvec = jnp.concatenate([c_ctx[None, :], c, jnp.zeros((8 - 1 - DEC_BATCH, D_MODEL), F32)], axis=0)
    mod = _adaln(cvec, w_ada, b_ada).reshape(DEPTH, 8, 1, 6 * D_MODEL)
    tab = _rope_table()

    lw = [_layer_weights(l, w_in, q_lora_norm, w_uq, kv_lora_norm, w_ukv, q_norm, k_norm, w_o, conv_dw, conv_dw_b,
                         conv_ln_g, conv_ln_b, w_pw2, w_out, norm1, norm2, w_group, b_group, w_expert, b_expert,
                         w1, w3, w2) for l in range(DEPTH)]

    cache_cat = jnp.concatenate(
        [cache_ckv, cache_krope, jnp.zeros((DEC_BATCH, DEPTH, PAST_LEN, G3_W - KV_LORA - QK_ROPE), F32)],
        axis=-1).transpose(1, 0, 2, 3)
    kc, vc = _cache_kv(cache_cat, jnp.stack([w["wkv"] for w in lw]), jnp.stack([w["kg"] for w in lw]))

    x = jnp.concatenate([x_prompt.reshape(T_CTX, D_MODEL), x_sample.reshape(T_LAT, D_MODEL)], axis=0)
    ckvs, kropes = [], []
    for l in range(DEPTH):
        w = lw[l]
        u, q, k, v, gates, ckv, kr = _front(x, mod[l], w["n1"], w["win"], w["qln"], w["wuq"], w["kvn"], w["wkv"],
                                            w["qg"], w["kg"], tab)
        ckvs.append(ckv[:T_CTX].reshape(BATCH, SEQ, KV_LORA))
        kropes.append(kr[:T_CTX].reshape(BATCH, SEQ, QK_ROPE))
        cg = _conv(u, gates, w["wdw"], w["bdw"], w["lng"], w["lnb"], w["wpw"])
        ah = jnp.concatenate([_attn_ctx(q, k, v), _attn_lat(q, k, v, kc[l], vc[l])], axis=0)
        x1, h2, comb = _post(x, cg, ah, gates, mod[l], w["wo"], w["wout"], w["n2"], w["wr"], w["br"])
        x = _moe(x1, h2, comb, mod[l], w["w13"], w["w2"])
    y_prompt = x[:T_CTX].reshape(BATCH, SEQ, D_MODEL)
    y_sample = x[T_CTX:].reshape(DEC_BATCH, DEC_SEQ, D_MODEL)
    return y_prompt, y_sample, jnp.stack(ckvs, axis=1), jnp.stack(kropes, axis=1)
```
